```python
import math
import jax, jax.numpy as jnp
from jax import lax
import numpy as np

D_MODEL = 1024
BATCH = 8
SEQ = 8192
DEPTH = 1
DEC_BATCH = 128
DEC_SEQ = 8
PAST_LEN = 8192
PAGE_SIZE = 128

D_MIX = D_MODEL
N_HEADS = 8
HEAD_DIM = 64
D_ATTN = N_HEADS * HEAD_DIM
IDX_HEADS = 4
IDX_DIM = 64
TOPK_MAX = 256
SSM_CH = 16
D_SSM = D_MIX - D_ATTN
SSM_GROUPS = D_SSM // SSM_CH
SSM_STATE = 64
N_BUCKETS = 32
MAX_DISTANCE = 128
Q_BLOCK = 128
RMS_EPS = 1e-6
DT_MIN = 1e-3
DT_MAX = 1e-1

SIZES = (D_ATTN, D_ATTN, D_ATTN, D_ATTN, IDX_HEADS * IDX_DIM, IDX_DIM, IDX_HEADS, D_SSM, D_SSM)
SPLITS = tuple(int(v) for v in np.cumsum(SIZES)[:-1])
N_IN = int(sum(SIZES))

kernel_name = 'hybrid_dsa_s5_parallel_step'

F32 = jnp.float32


def rms_norm(x, g):
    xf = x.astype(F32)
    y = xf * lax.rsqrt(jnp.mean(xf * xf, axis=-1, keepdims=True) + RMS_EPS)
    return (y * g.astype(F32)).astype(x.dtype)


def rel_bucket(rel):
    n = jnp.maximum(rel, 0)
    max_exact = N_BUCKETS // 2
    nf = jnp.maximum(n, 1).astype(F32)
    large = max_exact + (jnp.log(nf / max_exact) / math.log(MAX_DISTANCE / max_exact)
                         * (N_BUCKETS - max_exact)).astype(jnp.int32)
    large = jnp.minimum(large, N_BUCKETS - 1)
    return jnp.where(n < max_exact, n, large)


def take_rows(src, idx):
    return jax.vmap(lambda s, i: s[i])(src, idx)


def in_proj(x, norm_g, w_in, q_norm_g, k_norm_g):
    b_, s_ = x.shape[:2]
    hn = rms_norm(x, norm_g)
    z = jnp.einsum('bsd,de->bse', hn, w_in)
    q, k, v, g_att, qi, ki, wi, u, g_ssm = jnp.split(z, SPLITS, axis=-1)
    q = rms_norm(q.reshape(b_, s_, N_HEADS, HEAD_DIM), q_norm_g)
    k = rms_norm(k.reshape(b_, s_, N_HEADS, HEAD_DIM), k_norm_g)
    v = v.reshape(b_, s_, N_HEADS, HEAD_DIM)
    qi = qi.reshape(b_, s_, IDX_HEADS, IDX_DIM)
    return q, k, v, g_att, qi, ki, wi, u, g_ssm


def indexer_topk(qi, wi, ki, q_pos, n_sel):
    s = jnp.einsum('bqhd,bld->bqhl', qi.astype(F32), ki.astype(F32)) * (IDX_DIM ** -0.5)
    score = jnp.einsum('bqhl,bqh->bql', jax.nn.relu(s), wi.astype(F32)) * (IDX_HEADS ** -0.5)
    k_pos = jnp.arange(ki.shape[1])
    score = jnp.where(k_pos[None, None, :] <= q_pos[None, :, None], score, -jnp.inf)
    _, idx = lax.top_k(score, n_sel)
    return idx


def sparse_attend(q, k_sel, v_sel, idx, q_pos, rel_bias):
    logits = jnp.einsum('bqhd,bqkhd->bqhk', q.astype(F32), k_sel.astype(F32)) * (HEAD_DIM ** -0.5)
    rel = q_pos[None, :, None] - idx
    bias = rel_bias[rel_bucket(rel)].astype(F32)
    logits = logits + jnp.swapaxes(bias, -1, -2)
    logits = jnp.where((rel >= 0)[:, :, None, :], logits, -jnp.inf)
    p = jax.nn.softmax(logits, axis=-1)
    return jnp.einsum('bqhk,bqkhd->bqhd', p, v_sel.astype(F32)).astype(q.dtype)


def attn_prompt(q, k, v, qi, ki, wi, rel_bias):
    b_, s_ = q.shape[:2]
    n_sel = min(TOPK_MAX, s_ // 4)
    nb = s_ // Q_BLOCK

    def blockify(a):
        return jnp.moveaxis(a.reshape((b_, nb, Q_BLOCK) + a.shape[2:]), 1, 0)

    def one_block(args):
        qb, qib, wib, start = args
        q_pos = start + jnp.arange(Q_BLOCK)
        idx = indexer_topk(qib, wib, ki, q_pos, n_sel)
        return sparse_attend(qb, take_rows(k, idx), take_rows(v, idx), idx, q_pos, rel_bias)

    out = lax.map(one_block, (blockify(q), blockify(qi), blockify(wi), jnp.arange(nb) * Q_BLOCK))
    return jnp.moveaxis(out, 0, 1).reshape(b_, s_, D_ATTN)


def attn_sample(q, k, v, qi, ki, wi, ck, cv, cki, page_table, rel_bias):
    db, t = q.shape[:2]
    past = page_table.shape[1] * PAGE_SIZE
    n_sel = min(TOPK_MAX, (past + t) // 4)
    ki_past = cki[page_table].reshape(db, past, IDX_DIM)
    ki_all = jnp.concatenate([ki_past.astype(ki.dtype), ki], axis=1)
    q_pos = past + jnp.arange(t)
    idx = indexer_topk(qi, wi, ki_all, q_pos, n_sel)
    in_past = (idx < past)[..., None, None]
    pi = jnp.minimum(idx, past - 1)
    phys = take_rows(page_table, pi // PAGE_SIZE)
    off = pi % PAGE_SIZE
    ni = jnp.clip(idx - past, 0, t - 1)
    k_sel = jnp.where(in_past, ck[phys, off].astype(k.dtype), take_rows(k, ni))
    v_sel = jnp.where(in_past, cv[phys, off].astype(v.dtype), take_rows(v, ni))
    out = sparse_attend(q, k_sel, v_sel, idx, q_pos, rel_bias)
    return out.reshape(db, t, D_ATTN)


def cplx_combine(e1, e2):
    a1r, a1i, b1r, b1i = e1
    a2r, a2i, b2r, b2i = e2
    return (a2r * a1r - a2i * a1i,
            a2r * a1i + a2i * a1r,
            a2r * b1r - a2i * b1i + b2r,
            a2r * b1i + a2i * b1r + b2i)


def s5_branch(u, h0r, h0i, lam_re, lam_im, log_dt, b_re, b_im, c_re, c_im, d_skip, w_glu, b_glu):
    b_, s_ = u.shape[:2]
    uf = u.astype(F32).reshape(b_, s_, SSM_GROUPS, SSM_CH)
    dt = jnp.exp(log_dt.astype(F32))[:, None]
    lr, li = lam_re.astype(F32), lam_im.astype(F32)
    mag = jnp.exp(lr * dt)
    ar, ai = mag * jnp.cos(li * dt), mag * jnp.sin(li * dt)
    den = lr * lr + li * li
    fr = ((ar - 1.0) * lr + ai * li) / den
    fi = (ai * lr - (ar - 1.0) * li) / den
    br, bi = b_re.astype(F32), b_im.astype(F32)
    bbar_r = fr[..., None] * br - fi[..., None] * bi
    bbar_i = fr[..., None] * bi + fi[..., None] * br
    xr = jnp.einsum('bsgc,gpc->bsgp', uf, bbar_r)
    xi = jnp.einsum('bsgc,gpc->bsgp', uf, bbar_i)
    a_r = jnp.broadcast_to(ar, xr.shape)
    a_i = jnp.broadcast_to(ai, xr.shape)
    acc_r, acc_i, hr, hi = lax.associative_scan(cplx_combine, (a_r, a_i, xr, xi), axis=1)
    if h0r is not None:
        h0r = h0r.astype(F32)[:, None]
        h0i = h0i.astype(F32)[:, None]
        hr, hi = hr + acc_r * h0r - acc_i * h0i, hi + acc_r * h0i + acc_i * h0r
    y = (jnp.einsum('bsgp,gcp->bsgc', hr, c_re.astype(F32))
         - jnp.einsum('bsgp,gcp->bsgc', hi, c_im.astype(F32))
         + d_skip.astype(F32) * uf)
    y = jax.nn.gelu(y.reshape(b_, s_, D_SSM))
    y = y * jax.nn.sigmoid(jnp.einsum('bse,ef->bsf', y, w_glu.astype(F32)) + b_glu.astype(F32))
    return y.astype(u.dtype), hr[:, -1], hi[:, -1]


def merge_out(x, a, s, g_att, g_ssm, w_out):
    mix = jnp.concatenate([a * jax.nn.silu(g_att), s * jax.nn.silu(g_ssm)], axis=-1)
    return x + jnp.einsum('bse,ed->bsd', mix, w_out).astype(x.dtype)


def setup_inputs(seed: int = 0) -> dict:
    key = jax.random.key(seed)
    ks = jax.random.split(key, 32)
    n_pages = PAST_LEN // PAGE_SIZE
    n_used = DEC_BATCH * n_pages
    n_pool = n_used + max(1, n_used // 4)
    nrm = jax.random.normal
    page_table = jax.random.permutation(ks[0], n_pool)[:n_used].reshape(DEC_BATCH, n_pages).astype(jnp.int32)
    n_idx = jnp.arange(SSM_STATE, dtype=F32)
    return {
        'x_prompt': nrm(ks[1], (BATCH, SEQ, D_MODEL), F32),
        'x_sample': nrm(ks[2], (DEC_BATCH, DEC_SEQ, D_MODEL), F32),
        'cache_k': nrm(ks[3], (DEPTH, n_pool, PAGE_SIZE, N_HEADS, HEAD_DIM), F32),
        'cache_v': nrm(ks[4], (DEPTH, n_pool, PAGE_SIZE, N_HEADS, HEAD_DIM), F32),
        'cache_idx_k': nrm(ks[5], (DEPTH, n_pool, PAGE_SIZE, IDX_DIM), F32),
        'page_table': page_table,
        'state_ssm_re': 0.5 * nrm(ks[6], (DEPTH, DEC_BATCH, SSM_GROUPS, SSM_STATE), F32),
        'state_ssm_im': 0.5 * nrm(ks[7], (DEPTH, DEC_BATCH, SSM_GROUPS, SSM_STATE), F32),
        'rel_bias': 0.5 * nrm(ks[8], (N_BUCKETS, N_HEADS), F32),
        'norm_g': 1.0 + 0.05 * nrm(ks[9], (DEPTH, D_MODEL), F32),
        'w_in': nrm(ks[10], (DEPTH, D_MODEL, N_IN), F32) * D_MODEL ** -0.5,
        'q_norm_g': 1.0 + 0.05 * nrm(ks[11], (DEPTH, HEAD_DIM), F32),
        'k_norm_g': 1.0 + 0.05 * nrm(ks[12], (DEPTH, HEAD_DIM), F32),
        'lam_re': -0.5 + 0.01 * nrm(ks[13], (DEPTH, SSM_GROUPS, SSM_STATE), F32),
        'lam_im': math.pi * n_idx + 0.01 * nrm(ks[14], (DEPTH, SSM_GROUPS, SSM_STATE), F32),
        'log_dt': jax.random.uniform(ks[15], (DEPTH, SSM_GROUPS), F32, math.log(DT_MIN), math.log(DT_MAX)),
        'b_re': nrm(ks[16], (DEPTH, SSM_GROUPS, SSM_STATE, SSM_CH), F32) * (2.0 * SSM_CH) ** -0.5,
        'b_im': nrm(ks[17], (DEPTH, SSM_GROUPS, SSM_STATE, SSM_CH), F32) * (2.0 * SSM_CH) ** -0.5,
        'c_re': nrm(ks[18], (DEPTH, SSM_GROUPS, SSM_CH, SSM_STATE), F32) * (2.0 * SSM_STATE) ** -0.5,
        'c_im': nrm(ks[19], (DEPTH, SSM_GROUPS, SSM_CH, SSM_STATE), F32) * (2.0 * SSM_STATE) ** -0.5,
        'd_skip': nrm(ks[20], (DEPTH, SSM_GROUPS, SSM_CH), F32),
        'w_glu': nrm(ks[21], (DEPTH, D_SSM, D_SSM), F32) * D_SSM ** -0.5,
        'b_glu': 0.01 * nrm(ks[22], (DEPTH, D_SSM), F32),
        'w_out': nrm(ks[23], (DEPTH, D_MIX, D_MODEL), F32) * D_MIX ** -0.5,
    }


def reference(x_prompt, x_sample, cache_k, cache_v, cache_idx_k, page_table, state_ssm_re, state_ssm_im,
              rel_bias, norm_g, w_in, q_norm_g, k_norm_g, lam_re, lam_im, log_dt, b_re, b_im,
              c_re, c_im, d_skip, w_glu, b_glu, w_out):
    hp, hs = x_prompt, x_sample
    kp, vp, ikp, srp, sip = [], [], [], [], []
    ks_, vs_, iks, srs, sis = [], [], [], [], []
    for l in range(DEPTH):
        ssm_w = (lam_re[l], lam_im[l], log_dt[l], b_re[l], b_im[l], c_re[l], c_im[l], d_skip[l], w_glu[l], b_glu[l])
        q, k, v, g_att, qi, ki, wi, u, g_ssm = in_proj(hp, norm_g[l], w_in[l], q_norm_g[l], k_norm_g[l])
        a = attn_prompt(q, k, v, qi, ki, wi, rel_bias)
        s, hr, hi = s5_branch(u, None, None, *ssm_w)
        hp = merge_out(hp, a, s, g_att, g_ssm, w_out[l])
        kp.append(k); vp.append(v); ikp.append(ki); srp.append(hr); sip.append(hi)
        q, k, v, g_att, qi, ki, wi, u, g_ssm = in_proj(hs, norm_g[l], w_in[l], q_norm_g[l], k_norm_g[l])
        a = attn_sample(q, k, v, qi, ki, wi, cache_k[l], cache_v[l], cache_idx_k[l], page_table, rel_bias)
        s, hr, hi = s5_branch(u, state_ssm_re[l], state_ssm_im[l], *ssm_w)
        hs = merge_out(hs, a, s, g_att, g_ssm, w_out[l])
        ks_.append(k); vs_.append(v); iks.append(ki); srs.append(hr); sis.append(hi)
    return (hp, hs,
            jnp.stack(kp), jnp.stack(vp), jnp.stack(ikp), jnp.stack(srp), jnp.stack(sip),
            jnp.stack(ks_), jnp.stack(vs_), jnp.stack(iks), jnp.stack(srs), jnp.stack(sis))
```

```python
import functools
import math

import jax
import jax.numpy as jnp
from jax import lax
from jax.experimental import pallas as pl
from jax.experimental.pallas import tpu as pltpu

F32 = jnp.float32
BF16 = jnp.bfloat16

N_HEADS = 8
HEAD_DIM = 64
D_ATTN = N_HEADS * HEAD_DIM
IDX_HEADS = 4
IDX_DIM = 64
TOPK_MAX = 256
SSM_CH = 16
SSM_STATE = 64
N_BUCKETS = 32
MAX_DISTANCE = 128
RMS_EPS = 1e-6
PAGE_SIZE = 128

LANES = 128
NEG_INF = float("-inf")
M_INIT = -1e30
VMEM_LIMIT = 56 * 1024 * 1024

C_Q, C_K, C_V, C_GA = 0, 512, 1024, 1536
C_QI, C_KIWI, C_U, C_GS, C_END = 2048, 2304, 2432, 2944, 3456


def _dot(a, b):
    return jnp.dot(a, b, preferred_element_type=F32)


def _dot_nt(a, b):
    return lax.dot_general(a, b, (((1,), (1,)), ((), ())), preferred_element_type=F32)


def _split_bf16(x):
    hi = x.astype(BF16)
    lo = (x - hi.astype(F32)).astype(BF16)
    return hi, lo


def _lane_tile(x, reps):
    return x if reps == 1 else jnp.concatenate([x] * reps, axis=1)


def _in_proj_kernel(*refs, hi_prec):
    if hi_prec:
        x_ref, ng_ref, w_ref, wl_ref, ones_ref, qg_ref, kg_ref = refs[:7]
        outs = refs[7:]
    else:
        x_ref, ng_ref, w_ref, ones_ref, qg_ref, kg_ref = refs[:6]
        wl_ref = None
        outs = refs[6:]
    k_o, v_o, kiwi_o, ga_o, u_o, gs_o, qb_o, kb_o, vb_o, qib_o, kib_o = outs

    x = x_ref[...]
    ms = jnp.mean(x * x, axis=-1, keepdims=True)
    hn = x * lax.rsqrt(ms + RMS_EPS) * ng_ref[...]
    hh = hn.astype(BF16)
    hl = (hn - hh.astype(F32)).astype(BF16) if hi_prec else None

    def proj(c0, c1):
        r = _dot(hh, w_ref[:, c0:c1])
        if hi_prec:
            r = r + _dot(hl, w_ref[:, c0:c1]) + _dot(hh, wl_ref[:, c0:c1])
        return r

    ones = ones_ref[...]

    def head_norm(z, g):
        hi, lo = _split_bf16(z * z)
        msq = (_dot(hi, ones) + _dot(lo, ones)) * (1.0 / HEAD_DIM)
        return z * lax.rsqrt(msq + RMS_EPS) * g

    q = head_norm(proj(C_Q, C_K), qg_ref[...])
    qb_o[...] = (q * (HEAD_DIM ** -0.5)).astype(BF16)
    k = head_norm(proj(C_K, C_V), kg_ref[...])
    k_o[...] = k
    kb_o[...] = k.astype(BF16)
    v = proj(C_V, C_GA)
    v_o[...] = v
    vb_o[...] = v.astype(BF16)
    ga_o[...] = proj(C_GA, C_QI)
    qib_o[...] = (proj(C_QI, C_KIWI) * (IDX_DIM ** -0.5)).astype(BF16)
    kiwi = proj(C_KIWI, C_U)
    kiwi_o[...] = kiwi
    kib_o[...] = kiwi.astype(BF16)
    u_o[...] = proj(C_U, C_GS)
    gs_o[...] = proj(C_GS, C_END)


def _in_proj(x2d, norm_g, w_hi, w_lo, ones_bd, qg_t, kg_t, *, tm, hi_prec):
    t, d = x2d.shape
    assert t % tm == 0
    row = lambda n: pl.BlockSpec((tm, n), lambda i: (i, 0))
    const = lambda a: pl.BlockSpec(a.shape, lambda i: (0,) * a.ndim)
    ins = [x2d, norm_g, w_hi] + ([w_lo] if hi_prec else []) + [ones_bd, qg_t, kg_t]
    in_specs = [row(d)] + [const(a) for a in ins[1:]]
    widths = [(512, F32), (512, F32), (128, F32), (512, F32), (512, F32), (512, F32),
              (512, BF16), (512, BF16), (512, BF16), (256, BF16), (128, BF16)]
    return pl.pallas_call(
        functools.partial(_in_proj_kernel, hi_prec=hi_prec),
        grid=(t // tm,),
        in_specs=in_specs,
        out_specs=[row(n) for n, _ in widths],
        out_shape=[jax.ShapeDtypeStruct((t, n), dt) for n, dt in widths],
        compiler_params=pltpu.CompilerParams(dimension_semantics=("parallel",),
                                             vmem_limit_bytes=VMEM_LIMIT),
        name="in_proj_hi" if hi_prec else "in_proj",
    )(*ins)


_SIGN = -2147483648


def _decode_code(u):
    c = u ^ jnp.int32(_SIGN)
    bits = c ^ ((c >> 31) & jnp.int32(0x7FFFFFFF))
    return lax.bitcast_convert_type(bits, F32)


def _bisect_threshold(count_ge, rows, n_sel):
    def body(it, u):
        bit = jnp.left_shift(jnp.int32(1), jnp.int32(31) - it)
        u2 = u | bit
        cnt = count_ge(_decode_code(u2))
        return jnp.where(cnt >= float(n_sel), u2, u)

    u = lax.fori_loop(0, 32, body, jnp.zeros((rows, 1), jnp.int32))
    return _decode_code(u)


def _tri_inclusive(n):
    r = lax.broadcasted_iota(jnp.int32, (n, n), 0)
    c = lax.broadcasted_iota(jnp.int32, (n, n), 1)
    return (r <= c).astype(BF16)


def _attn_prompt_kernel(bfar_ref, qi_ref, wi_ref, kit_ref, q_ref, kt_ref, v_ref, bias_ref, o_ref,
                        s_ref, m_ref, l_ref, acc_ref, wb_ref, thr_ref, need_ref, *, tq, n_sel):
    tk = tq
    reps = tk // LANES
    rh = 128
    i = pl.program_id(1)
    nt = i + 1

    w = wi_ref[0]
    for h in range(IDX_HEADS):
        wb_ref[h] = jnp.broadcast_to(w[:, IDX_DIM + h:IDX_DIM + h + 1] * (IDX_HEADS ** -0.5), (tq, LANES))

    qi4 = qi_ref[0].reshape(IDX_HEADS * tq, IDX_DIM)

    def score_tile(j, carry):
        s4 = _dot(qi4, kit_ref[0, j])
        sc = jnp.zeros((tq, tk), F32)
        for h in range(IDX_HEADS):
            sc = sc + jnp.maximum(s4[h * tq:(h + 1) * tq], 0.0) * _lane_tile(wb_ref[h], reps)
        s_ref[j] = sc
        return carry

    lax.fori_loop(0, nt, score_tile, 0)
    row = lax.broadcasted_iota(jnp.int32, (tq, tk), 0)
    col = lax.broadcasted_iota(jnp.int32, (tq, tk), 1)
    causal = col <= row
    s_ref[i] = jnp.where(causal, s_ref[i], NEG_INF)

    def count_rows(r0, pred):
        def body(j, acc):
            st = s_ref[j, r0:r0 + rh, :]
            for c in range(reps):
                acc = acc + jnp.where(pred(st[:, c * LANES:(c + 1) * LANES]), 1.0, 0.0)
            return acc
        acc = lax.fori_loop(0, nt, body, jnp.zeros((rh, LANES), F32))
        return jnp.sum(acc, axis=1, keepdims=True)

    for r0 in range(0, tq, rh):
        qpos = i * tq + r0 + lax.broadcasted_iota(jnp.int32, (rh, 1), 0)

        @pl.when(i * tq + r0 + rh > n_sel)
        def _():
            def count_ge(cand):
                cb = jnp.broadcast_to(cand, (rh, LANES))
                return count_rows(r0, lambda s: s >= cb)
            t = _bisect_threshold(count_ge, rh, n_sel)
            t = jnp.where(qpos < n_sel, NEG_INF, t)
            tb = jnp.broadcast_to(t, (rh, LANES))
            thr_ref[r0:r0 + rh, :] = tb
            cnt_gt = count_rows(r0, lambda s: s > tb)
            need_ref[r0:r0 + rh, :] = jnp.broadcast_to(float(n_sel) - cnt_gt, (rh, LANES))

        @pl.when(i * tq + r0 + rh <= n_sel)
        def _():
            thr_ref[r0:r0 + rh, :] = jnp.full((rh, LANES), NEG_INF, F32)
            need_ref[r0:r0 + rh, :] = jnp.full((rh, LANES), float(n_sel), F32)

    tri = _tri_inclusive(tk)
    thr = _lane_tile(thr_ref[...], reps)
    need = _lane_tile(need_ref[...], reps)

    def mask_tile(j, carry):
        st = s_ref[j]
        eq = st == thr
        pref = _dot(eq.astype(BF16), tri)
        sel = (st > thr) | (eq & (pref + _lane_tile(carry, reps) <= need))
        sel = sel & (causal | (j < i))
        s_ref[j] = jnp.where(sel, 0.0, NEG_INF)
        return carry + jnp.broadcast_to(pref[:, tk - 1:tk], (tq, LANES))

    lax.fori_loop(0, nt, mask_tile, jnp.zeros((tq, LANES), F32))

    m_ref[...] = jnp.full(m_ref.shape, M_INIT, F32)
    l_ref[...] = jnp.zeros(l_ref.shape, F32)
    acc_ref[...] = jnp.zeros(acc_ref.shape, F32)

    def attend(j, bias_of_head):
        am = s_ref[j]
        for h in range(N_HEADS):
            lg = _dot(q_ref[0, h], kt_ref[0, h, j]) + bias_of_head(h) + am
            m_old = m_ref[h]
            m_new = jnp.maximum(m_old, jnp.max(lg, axis=1, keepdims=True))
            alpha = jnp.exp(m_old - m_new)
            p = jnp.exp(lg - _lane_tile(m_new, reps))
            psum = p[:, 0:LANES]
            for c in range(1, reps):
                psum = psum + p[:, c * LANES:(c + 1) * LANES]
            l_ref[h] = alpha * l_ref[h] + psum
            pair = (h // 2) * LANES
            acc_ref[h] = alpha * acc_ref[h] + _dot(p.astype(BF16), v_ref[0, j, :, pair:pair + LANES])
            m_ref[h] = m_new

    def far_tile(j, carry):
        attend(j, lambda h: bfar_ref[h])
        return carry

    lax.fori_loop(0, jnp.maximum(i - 1, 0), far_tile, 0)

    @pl.when(i > 0)
    def _():
        attend(i - 1, lambda h: bias_ref[1, h])

    attend(i, lambda h: bias_ref[0, h])

    lane = lax.broadcasted_iota(jnp.int32, (tq, LANES), 1)
    for pr in range(N_HEADS // 2):
        h0, h1 = 2 * pr, 2 * pr + 1
        o0 = acc_ref[h0] / jnp.sum(l_ref[h0], axis=1, keepdims=True)
        o1 = acc_ref[h1] / jnp.sum(l_ref[h1], axis=1, keepdims=True)
        o_ref[0, :, pr * LANES:(pr + 1) * LANES] = jnp.where(lane < HEAD_DIM, o0, o1)


def _rel_bucket(n):
    max_exact = N_BUCKETS // 2
    nf = jnp.maximum(n, 1).astype(F32)
    large = max_exact + (jnp.log(nf / max_exact) / math.log(MAX_DISTANCE / max_exact)
                         * (N_BUCKETS - max_exact)).astype(jnp.int32)
    large = jnp.minimum(large, N_BUCKETS - 1)
    return jnp.where(n < max_exact, n, large)


def _attn_prompt(qi4, kiwi, kit, q8, kt, v, rel_bias, *, tq, n_sel):
    b, _, s, _ = q8.shape
    nq = s // tq
    assert s % tq == 0 and tq + 1 >= MAX_DISTANCE and tq % 128 == 0
    r = jnp.arange(tq)[:, None]
    c = jnp.arange(tq)[None, :]
    n0 = jnp.maximum(r - c, 0)
    n1 = tq + r - c
    bias = jnp.stack([rel_bias[_rel_bucket(n0)], rel_bias[_rel_bucket(n1)]])
    bias = jnp.transpose(bias, (0, 3, 1, 2)).astype(F32)
    bfar = rel_bias[N_BUCKETS - 1].astype(F32)

    once = pl.Buffered(1)
    return pl.pallas_call(
        functools.partial(_attn_prompt_kernel, tq=tq, n_sel=n_sel),
        grid=(b, nq),
        in_specs=[
            pl.BlockSpec(memory_space=pltpu.SMEM),
            pl.BlockSpec((1, IDX_HEADS, tq, IDX_DIM), lambda bi, i: (bi, 0, i, 0)),
            pl.BlockSpec((1, tq, LANES), lambda bi, i: (bi, i, 0)),
            pl.BlockSpec((1, nq, IDX_DIM, tq), lambda bi, i: (bi, 0, 0, 0), pipeline_mode=once),
            pl.BlockSpec((1, N_HEADS, tq, HEAD_DIM), lambda bi, i: (bi, 0, i, 0)),
            pl.BlockSpec((1, N_HEADS, nq, HEAD_DIM, tq), lambda bi, i: (bi, 0, 0, 0, 0), pipeline_mode=once),
            pl.BlockSpec((1, nq, tq, D_ATTN), lambda bi, i: (bi, 0, 0, 0), pipeline_mode=once),
            pl.BlockSpec((2, N_HEADS, tq, tq), lambda bi, i: (0, 0, 0, 0), pipeline_mode=once),
        ],
        out_specs=pl.BlockSpec((1, tq, D_ATTN), lambda bi, i: (bi, i, 0)),
        out_shape=jax.ShapeDtypeStruct((b, s, D_ATTN), F32),
        scratch_shapes=[
            pltpu.VMEM((nq, tq, tq), F32),
            pltpu.VMEM((N_HEADS, tq, LANES), F32),
            pltpu.VMEM((N_HEADS, tq, LANES), F32),
            pltpu.VMEM((N_HEADS, tq, LANES), F32),
            pltpu.VMEM((IDX_HEADS, tq, LANES), F32),
            pltpu.VMEM((tq, LANES), F32),
            pltpu.VMEM((tq, LANES), F32),
        ],
        compiler_params=pltpu.CompilerParams(dimension_semantics=("parallel", "arbitrary"),
                                             vmem_limit_bytes=VMEM_LIMIT),
        name="attn_prompt",
    )(bfar, qi4, kiwi, kit, q8, kt, v, bias)


def _sample_scores_kernel(pt_ref, qi_ref, wi_ref, *refs, pages):
    page_refs, o_ref = refs[:pages], refs[pages]
    t = wi_ref.shape[1]
    keys = jnp.concatenate([r[0] for r in page_refs], axis=0).astype(BF16)
    s4 = _dot_nt(qi_ref[0], keys)
    w = wi_ref[0]
    sc = jnp.zeros((t, keys.shape[0]), F32)
    for h in range(IDX_HEADS):
        wh = w[:, IDX_DIM + h:IDX_DIM + h + 1] * (IDX_HEADS ** -0.5)
        sc = sc + jnp.maximum(s4[h * t:(h + 1) * t], 0.0) * wh
    o_ref[0, 0] = sc


def _sample_scores(page_table, qi4s, kiwi_s, cache_idx_k, *, pages):
    db, n_pages = page_table.shape
    t = kiwi_s.shape[1]
    npg = n_pages // pages
    assert n_pages % pages == 0
    w = pages * PAGE_SIZE

    def page_spec(p):
        return pl.BlockSpec((1, PAGE_SIZE, IDX_DIM),
                            lambda d, g, pt: (pt[d * n_pages + g * pages + p], 0, 0))

    grid_spec = pltpu.PrefetchScalarGridSpec(
        num_scalar_prefetch=1,
        grid=(db, npg),
        in_specs=[pl.BlockSpec((1, IDX_HEADS * t, IDX_DIM), lambda d, g, pt: (d, 0, 0)),
                  pl.BlockSpec((1, t, LANES), lambda d, g, pt: (d, 0, 0))]
                 + [page_spec(p) for p in range(pages)],
        out_specs=pl.BlockSpec((1, 1, t, w), lambda d, g, pt: (d, g, 0, 0)),
    )
    return pl.pallas_call(
        functools.partial(_sample_scores_kernel, pages=pages),
        grid_spec=grid_spec,
        out_shape=jax.ShapeDtypeStruct((db, npg, t, w), F32),
        compiler_params=pltpu.CompilerParams(dimension_semantics=("parallel", "arbitrary"),
                                             vmem_limit_bytes=VMEM_LIMIT),
        name="sample_scores",
    )(page_table.reshape(-1), qi4s, kiwi_s, *([cache_idx_k] * pages))


def _attn_sample_kernel(pt_ref, s_ref, qi_ref, wi_ref, kin_ref, qblk_ref, bfar_ref, blast_ref, bnew_ref,
                        knew_ref, vnew_ref, *refs, pages, n_sel):
    k_refs, v_refs = refs[:pages], refs[pages:2 * pages]
    o_ref = refs[2 * pages]
    mask_ref, maskn_ref, m_ref, l_ref, acc_ref = refs[2 * pages + 1:]
    g = pl.program_id(1)
    npg = pl.num_programs(1)
    t = wi_ref.shape[1]
    w = pages * PAGE_SIZE
    hq = N_HEADS * t

    @pl.when(g == 0)
    def _():
        s4 = _dot_nt(qi_ref[0], kin_ref[0])
        wv = wi_ref[0]
        sn = jnp.zeros((t, LANES), F32)
        for h in range(IDX_HEADS):
            wh = wv[:, IDX_DIM + h:IDX_DIM + h + 1] * (IDX_HEADS ** -0.5)
            sn = sn + jnp.maximum(s4[h * t:(h + 1) * t], 0.0) * wh
        rr = lax.broadcasted_iota(jnp.int32, (t, LANES), 0)
        cc = lax.broadcasted_iota(jnp.int32, (t, LANES), 1)
        sn = jnp.where(cc <= rr, sn, NEG_INF)
        sp = s_ref[0]

        def count(pred_p, pred_n):
            cp = jnp.sum(jnp.where(pred_p, 1.0, 0.0), axis=0)
            return (jnp.sum(cp, axis=1, keepdims=True)
                    + jnp.sum(jnp.where(pred_n, 1.0, 0.0), axis=1, keepdims=True))

        thr = _bisect_threshold(lambda cand: count(sp >= cand[None], sn >= cand), t, n_sel)
        need = float(n_sel) - count(sp > thr[None], sn > thr)
        n_ge = count(sp >= thr[None], sn >= thr)
        ties = jnp.max(n_ge) > float(n_sel)

        @pl.when(jnp.logical_not(ties))
        def _():
            mask_ref[...] = jnp.where(sp >= thr[None], 0.0, NEG_INF)
            maskn_ref[...] = jnp.where(sn >= thr, 0.0, NEG_INF)

        @pl.when(ties)
        def _():
            tri = _tri_inclusive(LANES)

            def chunk_mask(st, carry):
                eq = st == thr
                pref = _dot(eq.astype(BF16), tri)
                sel = (st > thr) | (eq & (pref + carry <= need))
                return jnp.where(sel, 0.0, NEG_INF), carry + pref[:, LANES - 1:LANES]

            def group(gi, carry):
                for c in range(w // LANES):
                    mk, carry = chunk_mask(s_ref[0, gi, :, c * LANES:(c + 1) * LANES], carry)
                    mask_ref[gi, :, c * LANES:(c + 1) * LANES] = mk
                return carry

            carry = lax.fori_loop(0, npg, group, jnp.zeros((t, 1), F32))
            mk, _ = chunk_mask(sn, carry)
            maskn_ref[...] = mk

        m_ref[...] = jnp.full(m_ref.shape, M_INIT, F32)
        l_ref[...] = jnp.zeros(l_ref.shape, F32)
        acc_ref[...] = jnp.zeros(acc_ref.shape, F32)

    def attend(kb, vb, bias, am):
        n = kb.shape[0]
        lg = _dot_nt(qblk_ref[0], kb) + bias
        lg = (lg.reshape(N_HEADS, t, n) + am[None]).reshape(hq, n)
        m_old = m_ref[...]
        m_new = jnp.maximum(m_old, jnp.max(lg, axis=1, keepdims=True))
        alpha = jnp.exp(m_old - m_new)
        p = jnp.exp(lg - m_new[:, 0:1])
        l_ref[...] = alpha * l_ref[...] + jnp.sum(p, axis=1, keepdims=True)
        acc_ref[...] = alpha[:, 0:1] * acc_ref[...] + _dot(p.astype(BF16), vb)
        m_ref[...] = m_new

    kb = jnp.concatenate([r[0] for r in k_refs], axis=0).astype(BF16)
    vb = jnp.concatenate([r[0] for r in v_refs], axis=0).astype(BF16)

    @pl.when(g < npg - 1)
    def _():
        attend(kb, vb, bfar_ref[...], mask_ref[g])

    @pl.when(g == npg - 1)
    def _():
        attend(kb, vb, blast_ref[...], mask_ref[g])
        attend(knew_ref[0], vnew_ref[0], bnew_ref[...], maskn_ref[...])
        o64 = (acc_ref[...] / l_ref[:, 0:1]).reshape(N_HEADS, t, D_ATTN)
        lane_head = lax.broadcasted_iota(jnp.int32, (t, D_ATTN), 1) // HEAD_DIM
        out = jnp.zeros((t, D_ATTN), F32)
        for h in range(N_HEADS):
            out = out + jnp.where(lane_head == h, o64[h], 0.0)
        o_ref[0] = out


def _attn_sample(page_table, scores, qi4s, kiwi_s, kin_pad, qblk, knew_pad, vnew_pad, cache_k, cache_v,
                 rel_bias, *, pages, n_sel):
    db, n_pages = page_table.shape
    t = kiwi_s.shape[1]
    npg = n_pages // pages
    w = pages * PAGE_SIZE
    past = n_pages * PAGE_SIZE
    hq = N_HEADS * t
    assert w >= MAX_DISTANCE

    tq = jnp.arange(t)[:, None]
    n_last = (past + tq) - (past - w + jnp.arange(w)[None, :])
    n_new = jnp.maximum(tq - jnp.arange(LANES)[None, :], 0)
    to_rows = lambda a: jnp.transpose(a, (2, 0, 1)).reshape(hq, a.shape[1]).astype(F32)
    blast = to_rows(rel_bias[_rel_bucket(n_last)])
    bnew = to_rows(rel_bias[_rel_bucket(n_new)])
    bfar = jnp.repeat(rel_bias[N_BUCKETS - 1].astype(F32), t)[:, None]

    def page_spec(p):
        return pl.BlockSpec((1, PAGE_SIZE, D_ATTN),
                            lambda d, g, pt: (pt[d * n_pages + g * pages + p], 0, 0))

    per_db = lambda shape: pl.BlockSpec((1,) + shape, lambda d, g, pt: (d,) + (0,) * len(shape))
    const = lambda a: pl.BlockSpec(a.shape, lambda d, g, pt: (0,) * a.ndim)
    grid_spec = pltpu.PrefetchScalarGridSpec(
        num_scalar_prefetch=1,
        grid=(db, npg),
        in_specs=[per_db((npg, t, w)), per_db((IDX_HEADS * t, LANES)), per_db((t, LANES)),
                  per_db((LANES, LANES)), per_db((hq, D_ATTN)),
                  const(bfar), const(blast), const(bnew),
                  per_db((LANES, D_ATTN)), per_db((LANES, D_ATTN))]
                 + [page_spec(p) for p in range(pages)] * 2,
        out_specs=pl.BlockSpec((1, t, D_ATTN), lambda d, g, pt: (d, 0, 0)),
        scratch_shapes=[pltpu.VMEM((npg, t, w), F32), pltpu.VMEM((t, LANES), F32),
                        pltpu.VMEM((hq, LANES), F32), pltpu.VMEM((hq, LANES), F32),
                        pltpu.VMEM((hq, D_ATTN), F32)],
    )
    return pl.pallas_call(
        functools.partial(_attn_sample_kernel, pages=pages, n_sel=n_sel),
        grid_spec=grid_spec,
        out_shape=jax.ShapeDtypeStruct((db, t, D_ATTN), F32),
        compiler_params=pltpu.CompilerParams(dimension_semantics=("parallel", "arbitrary"),
                                             vmem_limit_bytes=VMEM_LIMIT),
        name="attn_sample",
    )(page_table.reshape(-1), scores, qi4s, kiwi_s, kin_pad, qblk, bfar, blast, bnew, knew_pad, vnew_pad,
      *([cache_k] * pages), *([cache_v] * pages))


def _s5_kernel(*refs, hi_prec, cw):
    (u_ref, h0r_ref, h0i_ref, ar_ref, ai_ref, bre_ref, bim_ref, brel_ref, biml_ref, cre_ref, cim_ref,
     crel_ref, ciml_ref, d_ref, wg_ref, bg_ref, s_ref, hr_ref, hi_ref, xr_ref, xi_ref) = refs
    tc, bt, dssm = u_ref.shape
    nst = xr_ref.shape[2]
    c = pl.program_id(0)

    @pl.when(c == 0)
    def _():
        hr_ref[...] = h0r_ref[...]
        hi_ref[...] = h0i_ref[...]

    u = u_ref[...].reshape(tc * bt, dssm)
    uh, ul = _split_bf16(u)

    kch = bre_ref.shape[0]
    sw = nst // kch
    for kc in range(kch):
        a_h = uh[:, kc * LANES:(kc + 1) * LANES]
        xr = _dot(a_h, bre_ref[kc])
        xi = _dot(a_h, bim_ref[kc])
        if hi_prec:
            a_l = ul[:, kc * LANES:(kc + 1) * LANES]
            xr = xr + _dot(a_l, bre_ref[kc]) + _dot(a_h, brel_ref[kc])
            xi = xi + _dot(a_l, bim_ref[kc]) + _dot(a_h, biml_ref[kc])
        xr_ref[:, :, kc * sw:(kc + 1) * sw] = xr.reshape(tc, bt, sw)
        xi_ref[:, :, kc * sw:(kc + 1) * sw] = xi.reshape(tc, bt, sw)

    for c0 in range(0, nst, cw):
        ar = jnp.broadcast_to(ar_ref[:, c0:c0 + cw], (bt, cw))
        ai = jnp.broadcast_to(ai_ref[:, c0:c0 + cw], (bt, cw))

        def step(t, carry):
            hr, hi = carry
            nr = ar * hr - ai * hi + xr_ref[t, :, c0:c0 + cw]
            ni = ar * hi + ai * hr + xi_ref[t, :, c0:c0 + cw]
            xr_ref[t, :, c0:c0 + cw] = nr
            xi_ref[t, :, c0:c0 + cw] = ni
            return nr, ni

        hr, hi = lax.fori_loop(0, tc, step, (hr_ref[:, c0:c0 + cw], hi_ref[:, c0:c0 + cw]))
        hr_ref[:, c0:c0 + cw] = hr
        hi_ref[:, c0:c0 + cw] = hi

    hs_r = xr_ref[...].reshape(tc * bt, nst)
    hs_i = xi_ref[...].reshape(tc * bt, nst)
    ys = []
    for kc in range(kch):
        pr = hs_r[:, kc * sw:(kc + 1) * sw]
        pi = hs_i[:, kc * sw:(kc + 1) * sw]
        if hi_prec:
            prh, prl = _split_bf16(pr)
            pih, pil = _split_bf16(pi)
            y = (_dot(prh, cre_ref[kc]) + _dot(prl, cre_ref[kc]) + _dot(prh, crel_ref[kc])
                 - _dot(pih, cim_ref[kc]) - _dot(pil, cim_ref[kc]) - _dot(pih, ciml_ref[kc]))
        else:
            y = _dot(pr.astype(BF16), cre_ref[kc]) - _dot(pi.astype(BF16), cim_ref[kc])
        ys.append(y)
    y = jnp.concatenate(ys, axis=1) + d_ref[...] * u
    y = jax.nn.gelu(y)
    gate = jax.nn.sigmoid(_dot(y.astype(BF16), wg_ref[...]) + bg_ref[...])
    s_ref[...] = (y * gate).reshape(tc, bt, dssm)


def _s5(u_tb, h0r, h0i, prm, *, tc, hi_prec):
    s, bt, dssm = u_tb.shape
    nst = h0r.shape[1]
    assert s % tc == 0
    cw = 512 if bt <= 8 else 128
    const = lambda a: pl.BlockSpec(a.shape, lambda c: (0,) * a.ndim)
    ins = [u_tb, h0r, h0i] + list(prm)
    return pl.pallas_call(
        functools.partial(_s5_kernel, hi_prec=hi_prec, cw=cw),
        grid=(s // tc,),
        in_specs=[pl.BlockSpec((tc, bt, dssm), lambda c: (c, 0, 0))] + [const(a) for a in ins[1:]],
        out_specs=[pl.BlockSpec((tc, bt, dssm), lambda c: (c, 0, 0)),
                   pl.BlockSpec((bt, nst), lambda c: (0, 0)),
                   pl.BlockSpec((bt, nst), lambda c: (0, 0))],
        out_shape=[jax.ShapeDtypeStruct((s, bt, dssm), F32),
                   jax.ShapeDtypeStruct((bt, nst), F32),
                   jax.ShapeDtypeStruct((bt, nst), F32)],
        scratch_shapes=[pltpu.VMEM((tc, bt, nst), F32), pltpu.VMEM((tc, bt, nst), F32)],
        compiler_params=pltpu.CompilerParams(dimension_semantics=("arbitrary",),
                                             vmem_limit_bytes=VMEM_LIMIT),
        name="s5_hi" if hi_prec else "s5",
    )(*ins)


def _s5_params(lam_re, lam_im, log_dt, b_re, b_im, c_re, c_im, d_skip, w_glu, b_glu):
    g, p = lam_re.shape
    ch = b_re.shape[2]
    dt = jnp.exp(log_dt.astype(F32))[:, None]
    lr, li = lam_re.astype(F32), lam_im.astype(F32)
    mag = jnp.exp(lr * dt)
    ar, ai = mag * jnp.cos(li * dt), mag * jnp.sin(li * dt)
    den = lr * lr + li * li
    fr = ((ar - 1.0) * lr + ai * li) / den
    fi = (ai * lr - (ar - 1.0) * li) / den
    br, bi = b_re.astype(F32), b_im.astype(F32)
    bbar_r = fr[..., None] * br - fi[..., None] * bi
    bbar_i = fr[..., None] * bi + fi[..., None] * br
    gpc = LANES // ch
    kch = g // gpc
    eye = jnp.eye(gpc, dtype=F32)

    def pack_b(bb):
        bb = bb.reshape(kch, gpc, p, ch)
        return jnp.einsum('kgpc,gh->kgchp', bb, eye).reshape(kch, gpc * ch, gpc * p)

    def pack_c(cc):
        cc = cc.astype(F32).reshape(kch, gpc, ch, p)
        return jnp.einsum('kgcp,gh->kgphc', cc, eye).reshape(kch, gpc * p, gpc * ch)

    def hl(x):
        hi = x.astype(BF16)
        return hi, (x - hi.astype(F32)).astype(BF16)

    bre, brel = hl(pack_b(bbar_r))
    bim, biml = hl(pack_b(bbar_i))
    cre, crel = hl(pack_c(c_re))
    cim, ciml = hl(pack_c(c_im))
    return (ar.reshape(1, g * p), ai.reshape(1, g * p), bre, bim, brel, biml, cre, cim, crel, ciml,
            d_skip.astype(F32).reshape(1, g * ch), w_glu.astype(BF16), b_glu.astype(F32).reshape(1, -1))


def _merge_kernel(x_ref, a_ref, s_ref, ga_ref, gs_ref, wa_ref, ws_ref, o_ref):
    ma = (a_ref[...] * jax.nn.silu(ga_ref[...])).astype(BF16)
    msm = (s_ref[...] * jax.nn.silu(gs_ref[...])).astype(BF16)
    o_ref[...] = x_ref[...] + _dot(ma, wa_ref[...]) + _dot(msm, ws_ref[...])


def _merge(x2d, a, s, ga, gs, w_a, w_s, *, tm):
    t, d = x2d.shape
    row = lambda n: pl.BlockSpec((tm, n), lambda i: (i, 0))
    const = lambda a_: pl.BlockSpec(a_.shape, lambda i: (0,) * a_.ndim)
    return pl.pallas_call(
        _merge_kernel,
        grid=(t // tm,),
        in_specs=[row(d), row(a.shape[1]), row(s.shape[1]), row(ga.shape[1]), row(gs.shape[1]),
                  const(w_a), const(w_s)],
        out_specs=row(d),
        out_shape=jax.ShapeDtypeStruct((t, d), F32),
        compiler_params=pltpu.CompilerParams(dimension_semantics=("parallel",),
                                             vmem_limit_bytes=VMEM_LIMIT),
        name="merge",
    )(x2d, a, s, ga, gs, w_a, w_s)


def _pack_w_in(w):
    d = w.shape[0]
    o_qi = 4 * D_ATTN
    o_ki = o_qi + IDX_HEADS * IDX_DIM
    o_wi = o_ki + IDX_DIM
    o_u = o_wi + IDX_HEADS
    pad = jnp.zeros((d, LANES - IDX_DIM - IDX_HEADS), w.dtype)
    return jnp.concatenate([w[:, :o_wi + IDX_HEADS], pad, w[:, o_u:]], axis=1)


def _layer(x_prompt, x_sample, cache_k, cache_v, cache_idx_k, page_table, state_re, state_im, rel_bias,
           norm_g, w_in, q_norm_g, k_norm_g, ssm_w, w_out):
    b, s, d = x_prompt.shape
    db, t, _ = x_sample.shape
    g_ssm, p_ssm = ssm_w[0].shape
    nst = g_ssm * p_ssm

    wp = _pack_w_in(w_in.astype(F32))
    w_hi = wp.astype(BF16)
    w_lo = (wp - w_hi.astype(F32)).astype(BF16)
    ones_bd = jnp.kron(jnp.eye(N_HEADS, dtype=F32), jnp.ones((HEAD_DIM, HEAD_DIM), F32)).astype(BF16)
    ng = norm_g.astype(F32).reshape(1, d)
    qg_t = jnp.tile(q_norm_g.astype(F32), N_HEADS).reshape(1, D_ATTN)
    kg_t = jnp.tile(k_norm_g.astype(F32), N_HEADS).reshape(1, D_ATTN)
    prm = _s5_params(*ssm_w)
    w_a = w_out[:D_ATTN].astype(BF16)
    w_s = w_out[D_ATTN:].astype(BF16)

    tm = 512 if (b * s) % 512 == 0 else 256
    (k_p, v_p, kiwi_p, ga_p, u_p, gs_p, qb, kb, vb, qib, kib) = _in_proj(
        x_prompt.reshape(b * s, d), ng, w_hi, w_lo, ones_bd, qg_t, kg_t, tm=tm, hi_prec=False)
    tq = 256
    nq = s // tq
    n_sel = min(TOPK_MAX, s // 4)
    qi4 = jnp.transpose(qib.reshape(b, s, IDX_HEADS, IDX_DIM), (0, 2, 1, 3))
    q8 = jnp.transpose(qb.reshape(b, s, N_HEADS, HEAD_DIM), (0, 2, 1, 3))
    kit = jnp.transpose(kib[:, :IDX_DIM].reshape(b, nq, tq, IDX_DIM), (0, 1, 3, 2))
    kt = jnp.transpose(kb.reshape(b, nq, tq, N_HEADS, HEAD_DIM), (0, 3, 1, 4, 2))
    a_p = _attn_prompt(qi4, kiwi_p.reshape(b, s, LANES), kit, q8, kt, vb.reshape(b, nq, tq, D_ATTN),
                       rel_bias, tq=tq, n_sel=n_sel)
    u_tb = jnp.transpose(u_p.reshape(b, s, -1), (1, 0, 2))
    zeros = jnp.zeros((b, nst), F32)
    s_tb, hr_p, hi_p = _s5(u_tb, zeros, zeros, prm, tc=64 if s % 64 == 0 else 8, hi_prec=False)
    s_p = jnp.transpose(s_tb, (1, 0, 2)).reshape(b * s, -1)
    y_p = _merge(x_prompt.reshape(b * s, d), a_p.reshape(b * s, D_ATTN), s_p, ga_p, gs_p, w_a, w_s, tm=tm)

    n_tok = db * t
    tms = 256 if n_tok % 256 == 0 else n_tok
    (k_s, v_s, kiwi_s, ga_s, u_s, gs_s, qbs, kbs, vbs, qibs, kibs) = _in_proj(
        x_sample.reshape(n_tok, d), ng, w_hi, w_lo, ones_bd, qg_t, kg_t, tm=tms, hi_prec=True)
    n_pages = page_table.shape[1]
    past = n_pages * PAGE_SIZE
    n_sel_s = min(TOPK_MAX, (past + t) // 4)
    pages = 8 if n_pages % 8 == 0 else n_pages
    qi4s = jnp.transpose(qibs.reshape(db, t, IDX_HEADS, IDX_DIM), (0, 2, 1, 3)).reshape(db, IDX_HEADS * t, IDX_DIM)
    kiwi_s3 = kiwi_s.reshape(db, t, LANES)
    scores = _sample_scores(page_table, qi4s, kiwi_s3, cache_idx_k, pages=pages)
    qi4s_pad = jnp.pad(qi4s, ((0, 0), (0, 0), (0, LANES - IDX_DIM)))
    lane_ok = (jnp.arange(LANES) < IDX_DIM)[None, None, :]
    kin_pad = jnp.pad(jnp.where(lane_ok, kibs.reshape(db, t, LANES), 0), ((0, 0), (0, LANES - t), (0, 0)))
    q4 = qbs.reshape(db, t, N_HEADS, HEAD_DIM)
    qblk = jnp.einsum('bthd,hg->bhtgd', q4, jnp.eye(N_HEADS, dtype=BF16)).reshape(db, N_HEADS * t, D_ATTN)
    knew_pad = jnp.pad(kbs.reshape(db, t, D_ATTN), ((0, 0), (0, LANES - t), (0, 0)))
    vnew_pad = jnp.pad(vbs.reshape(db, t, D_ATTN), ((0, 0), (0, LANES - t), (0, 0)))
    a_s = _attn_sample(page_table, scores, qi4s_pad, kiwi_s3, kin_pad, qblk, knew_pad, vnew_pad,
                       cache_k.reshape(-1, PAGE_SIZE, D_ATTN), cache_v.reshape(-1, PAGE_SIZE, D_ATTN),
                       rel_bias, pages=pages, n_sel=n_sel_s)
    us_tb = jnp.transpose(u_s.reshape(db, t, -1), (1, 0, 2))
    ss_tb, hr_s, hi_s = _s5(us_tb, state_re.reshape(db, nst).astype(F32), state_im.reshape(db, nst).astype(F32),
                            prm, tc=t, hi_prec=True)
    s_s = jnp.transpose(ss_tb, (1, 0, 2)).reshape(n_tok, -1)
    y_s = _merge(x_sample.reshape(n_tok, d), a_s.reshape(n_tok, D_ATTN), s_s, ga_s, gs_s, w_a, w_s, tm=tms)

    return (y_p.reshape(b, s, d), y_s.reshape(db, t, d),
            k_p.reshape(b, s, N_HEADS, HEAD_DIM), v_p.reshape(b, s, N_HEADS, HEAD_DIM),
            kiwi_p[:, :IDX_DIM].reshape(b, s, IDX_DIM),
            hr_p.reshape(b, g_ssm, p_ssm), hi_p.reshape(b, g_ssm, p_ssm),
            k_s.reshape(db, t, N_HEADS, HEAD_DIM), v_s.reshape(db, t, N_HEADS, HEAD_DIM),
            kiwi_s[:, :IDX_DIM].reshape(db, t, IDX_DIM),
            hr_s.reshape(db, g_ssm, p_ssm), hi_s.reshape(db, g_ssm, p_ssm))


def kernel(x_prompt, x_sample, cache_k, cache_v, cache_idx_k, page_table, state_ssm_re, state_ssm_im,
           rel_bias, norm_g, w_in, q_norm_g, k_norm_g, lam_re, lam_im, log_dt, b_re, b_im, c_re, c_im,
           d_skip, w_glu, b_glu, w_out):
    depth = w_in.shape[0]
    assert depth == 1, "parallel-head step is specified for a single layer"
    ssm_w = (lam_re[0], lam_im[0], log_dt[0], b_re[0], b_im[0], c_re[0], c_im[0], d_skip[0], w_glu[0], b_glu[0])
    outs = _layer(x_prompt, x_sample, cache_k[0], cache_v[0], cache_idx_k[0], page_table,
                  state_ssm_re[0], state_ssm_im[0], rel_bias, norm_g[0], w_in[0], q_norm_g[0], k_norm_g[0],
                  ssm_w, w_out[0])
    y_p, y_s = outs[0], outs[1]
    return (y_p, y_s) + tuple(o[None] for o in outs[2:])
```
